```python
import math
import jax, jax.numpy as jnp
from jax import lax
import numpy as np

D_MODEL = 1024
BATCH = 32
SEQ = 2048
DEPTH = 4
DEC_BATCH = 2
DEC_SEQ = 8192
PAST_LEN = 128

HEAD_DIM = 64
EPS = 1e-6
NEG = -1e30
A_HEADS = 16
A_KV_HEADS = 4
A_RADIUS = 128
B_PAIRS = ((128, 1), (512, 4), (2048, 16))
B_HEADS = 8
C_HEADS = 8
C_QK_DIM = 64
C_V_DIM = 2 * C_QK_DIM
C_QBLOCK = 128
D_HEADS = 4
D_K_DIM = 128
D_V_DIM = 128
D_GATE_RANK = 16
D_GATE_TAU = 16.0
D_CHUNK = 64

N_EVEN = (DEPTH + 1) // 2
N_ODD = DEPTH // 2
A_WIDTH = A_HEADS * HEAD_DIM
A_KV_WIDTH = A_KV_HEADS * HEAD_DIM
B_WIDTH = B_HEADS * HEAD_DIM
B_QKV_WIDTH = len(B_PAIRS) * B_WIDTH
EVEN_SPLITS = (A_WIDTH, A_KV_WIDTH, A_KV_WIDTH, A_WIDTH, B_QKV_WIDTH, B_QKV_WIDTH, B_QKV_WIDTH, B_WIDTH)
EVEN_COLS = sum(EVEN_SPLITS)
EVEN_WIDTH = A_WIDTH + B_WIDTH
C_QK_WIDTH = C_HEADS * 2 * C_QK_DIM
C_WIDTH = C_HEADS * C_V_DIM
D_KEY_WIDTH = D_HEADS * D_K_DIM
D_WIDTH = D_HEADS * D_V_DIM
ODD_SPLITS = (C_QK_WIDTH, C_QK_WIDTH, C_WIDTH, C_WIDTH, D_KEY_WIDTH, D_KEY_WIDTH, D_WIDTH, D_WIDTH, 2 * D_GATE_RANK)
ODD_COLS = sum(ODD_SPLITS)
ODD_WIDTH = C_WIDTH + D_WIDTH

kernel_name = "hybrid_bidir_encoder_trunk"


def rmsnorm(x, g):
    x32 = x.astype(jnp.float32)
    y = x32 * lax.rsqrt(jnp.mean(x32 * x32, axis=-1, keepdims=True) + EPS) * g.astype(jnp.float32)
    return y.astype(x.dtype)


def split_cols(z, sizes):
    idx = np.cumsum(np.array(sizes))[:-1].tolist()
    return jnp.split(z, idx, axis=-1)


def alibi_slopes(n):
    return jnp.asarray([2.0 ** (-8.0 * (i + 1) / n) for i in range(n)], dtype=jnp.float32)


def banded_attention(q, k, v, slopes, radius, stride, sink=None):
    Bsz, Hq, T, hd = q.shape
    Hkv = k.shape[1]
    G = Hq // Hkv
    blk = radius
    n = -(-T // blk)
    Tp = n * blk
    pad = Tp - T
    qb = jnp.pad(q, ((0, 0), (0, 0), (0, pad), (0, 0))).reshape(Bsz, Hkv, G, n, blk, hd)

    def windows(a):
        ap = jnp.pad(a, ((0, 0), (0, 0), (blk, blk + pad), (0, 0))).reshape(Bsz, Hkv, n + 2, blk, hd)
        return jnp.concatenate([ap[:, :, :-2], ap[:, :, 1:-1], ap[:, :, 2:]], axis=-2)

    kw, vw = windows(k), windows(v)
    s = jnp.einsum('bkgnqd,bknsd->bkgnqs', qb, kw, preferred_element_type=jnp.float32) * (hd ** -0.5)
    qpos = jnp.arange(Tp).reshape(n, blk)[:, :, None]
    kpos = (jnp.arange(n)[:, None] * blk - blk + jnp.arange(3 * blk)[None, :])[:, None, :]
    dist = jnp.abs(qpos - kpos)
    valid = (dist <= radius) & (kpos >= 0) & (kpos < T)
    s = s - slopes.reshape(Hkv, G)[None, :, :, None, None, None] * (stride * dist).astype(jnp.float32)
    s = jnp.where(valid, s, NEG)
    m = jnp.max(s, axis=-1)
    if sink is not None:
        sk = sink.astype(jnp.float32).reshape(Hkv, G)[None, :, :, None, None]
        m = jnp.maximum(m, sk)
    p = jnp.exp(s - m[..., None])
    denom = jnp.sum(p, axis=-1)
    if sink is not None:
        denom = denom + jnp.exp(sk - m)
    o = jnp.einsum('bkgnqs,bknsd->bkgnqd', p.astype(v.dtype), vw, preferred_element_type=jnp.float32) / denom[..., None]
    lse = m + jnp.log(denom)
    o = o.reshape(Bsz, Hq, Tp, hd)[:, :, :T]
    lse = lse.reshape(Bsz, Hq, Tp)[:, :, :T]
    return o, lse


def dilated_mixture(q, k, v, slopes):
    P, Bsz, H, T, hd = q.shape
    outs, lses = [], []
    for p, (window, dil) in enumerate(B_PAIRS):
        radius = window // (2 * dil)
        L = T // dil

        def gather(a):
            return a.reshape(Bsz, H, L, dil, hd).transpose(0, 3, 1, 2, 4).reshape(Bsz * dil, H, L, hd)

        o, lse = banded_attention(gather(q[p]), gather(k[p]), gather(v[p]), slopes, radius, dil)
        outs.append(o.reshape(Bsz, dil, H, L, hd).transpose(0, 2, 3, 1, 4).reshape(Bsz, H, T, hd))
        lses.append(lse.reshape(Bsz, dil, H, L).transpose(0, 2, 3, 1).reshape(Bsz, H, T))
    wts = jax.nn.softmax(jnp.stack(lses, axis=0), axis=0)
    return jnp.einsum('pbht,pbhtd->bhtd', wts, jnp.stack(outs, axis=0))


def diff_attention(q, k, v, slopes, lam):
    Bsz, H, _, T, dk = q.shape
    nq = T // C_QBLOCK
    qb = jnp.moveaxis(q.reshape(Bsz, H, 2, nq, C_QBLOCK, dk), 3, 0)
    starts = jnp.arange(nq) * C_QBLOCK
    kpos = jnp.arange(T)

    def block(args):
        qi, start = args
        s = jnp.einsum('bhiqd,bhisd->bhiqs', qi, k, preferred_element_type=jnp.float32) * (dk ** -0.5)
        qpos = start + jnp.arange(C_QBLOCK)
        dist = jnp.abs(qpos[:, None] - kpos[None, :]).astype(jnp.float32)
        s = s - slopes[None, :, None, None, None] * dist[None, None, None]
        p = jax.nn.softmax(s, axis=-1)
        a = p[:, :, 0] - lam * p[:, :, 1]
        return jnp.einsum('bhqs,bhsd->bhqd', a.astype(v.dtype), v, preferred_element_type=jnp.float32)

    o = lax.map(block, (qb, starts))
    return jnp.moveaxis(o, 0, 2).reshape(Bsz, H, T, v.shape[-1])


def gla_direction(q, k, v, log_a, strict):
    Bsz, H, T, dk = q.shape
    dv = v.shape[-1]
    C = D_CHUNK
    n = T // C
    q = q.reshape(Bsz, H, n, C, dk)
    k = k.reshape(Bsz, H, n, C, dk)
    v = v.reshape(Bsz, H, n, C, dv)
    b = jnp.cumsum(log_a.reshape(Bsz, H, n, C, dk), axis=3)
    b_last = b[:, :, :, -1:, :]
    q_e = q * jnp.exp(b)
    k_e = k * jnp.exp(-b)
    k_s = k * jnp.exp(b_last - b)
    mask = jnp.tril(jnp.ones((C, C), jnp.float32), k=-1 if strict else 0)
    a = jnp.einsum('bhncd,bhnsd->bhncs', q_e, k_e) * mask
    intra = jnp.einsum('bhncs,bhnsv->bhncv', a, v)
    kv = jnp.einsum('bhncd,bhncv->bhndv', k_s, v)
    decay = jnp.exp(b_last[:, :, :, 0, :])

    def step(S, inp):
        dec, kv_c = inp
        return dec[..., None] * S + kv_c, S

    S0 = jnp.zeros((Bsz, H, dk, dv), jnp.float32)
    _, S_prev = lax.scan(step, S0, (jnp.moveaxis(decay, 2, 0), jnp.moveaxis(kv, 2, 0)))
    S_prev = jnp.moveaxis(S_prev, 0, 2)
    inter = jnp.einsum('bhncd,bhndv->bhncv', q_e, S_prev)
    return (intra + inter).reshape(Bsz, H, T, dv)


def even_layer(x, g, w_in, w_out, sink):
    Bsz, T, _ = x.shape
    h = rmsnorm(x, g)
    z = jnp.einsum('btd,dc->btc', h, w_in)
    qa, ka, va, ga, qb, kb, vb, gb = split_cols(z, EVEN_SPLITS)

    def heads(a, nh):
        return a.reshape(Bsz, T, nh, HEAD_DIM).transpose(0, 2, 1, 3)

    oa, _ = banded_attention(heads(qa, A_HEADS), heads(ka, A_KV_HEADS), heads(va, A_KV_HEADS),
                             alibi_slopes(A_HEADS), A_RADIUS, 1, sink)
    oa = oa.transpose(0, 2, 1, 3).reshape(Bsz, T, A_WIDTH)

    def groups(a):
        return a.reshape(Bsz, T, len(B_PAIRS), B_HEADS, HEAD_DIM).transpose(2, 0, 3, 1, 4)

    ob = dilated_mixture(groups(qb), groups(kb), groups(vb), alibi_slopes(B_HEADS))
    ob = ob.transpose(0, 2, 1, 3).reshape(Bsz, T, B_WIDTH)
    y = jnp.concatenate([oa * jax.nn.silu(ga.astype(jnp.float32)), ob * jax.nn.silu(gb.astype(jnp.float32))], axis=-1)
    return x + jnp.einsum('btc,cd->btd', y.astype(x.dtype), w_out).astype(x.dtype)


def odd_layer(x, g, w_in, w_out, lam_p, subln_g, w_gate2, b_gate, gla_g, layer_idx):
    Bsz, T, _ = x.shape
    h = rmsnorm(x, g)
    z = jnp.einsum('btd,dc->btc', h, w_in)
    qc, kc, vc, gc, qd, kd, vd, gd, ad = split_cols(z, ODD_SPLITS)

    def qk_pair(a):
        return a.reshape(Bsz, T, C_HEADS, 2, C_QK_DIM).transpose(0, 2, 3, 1, 4)

    vch = vc.reshape(Bsz, T, C_HEADS, C_V_DIM).transpose(0, 2, 1, 3)
    lam_init = 0.8 - 0.6 * math.exp(-0.3 * layer_idx)
    lp = lam_p.astype(jnp.float32)
    lam = jnp.exp(jnp.sum(lp[0] * lp[1])) - jnp.exp(jnp.sum(lp[2] * lp[3])) + lam_init
    oc = diff_attention(qk_pair(qc), qk_pair(kc), vch, alibi_slopes(C_HEADS), lam)
    oc = rmsnorm(oc, subln_g) * (1.0 - lam_init)
    oc = oc.transpose(0, 2, 1, 3).reshape(Bsz, T, C_WIDTH)

    rank = ad.astype(jnp.float32).reshape(Bsz, T, 2, D_GATE_RANK)
    log_a = jax.nn.log_sigmoid(jnp.einsum('btpr,prk->pbtk', rank, w_gate2.astype(jnp.float32))
                               + b_gate.astype(jnp.float32)[:, None, None, :]) / D_GATE_TAU

    def dheads(a, dh):
        return a.astype(jnp.float32).reshape(Bsz, T, D_HEADS, dh).transpose(0, 2, 1, 3)

    qh = dheads(qd, D_K_DIM) * (D_K_DIM ** -0.5)
    kh = dheads(kd, D_K_DIM)
    vh = dheads(vd, D_V_DIM)
    la_f = dheads(log_a[0], D_K_DIM)
    la_b = dheads(log_a[1], D_K_DIM)
    fl = lambda a: jnp.flip(a, axis=2)
    o_f = gla_direction(qh, kh, vh, la_f, strict=False)
    o_b = fl(gla_direction(fl(qh), fl(kh), fl(vh), fl(la_b), strict=True))
    od = rmsnorm(o_f + o_b, gla_g).transpose(0, 2, 1, 3).reshape(Bsz, T, D_WIDTH)

    y = jnp.concatenate([oc * jax.nn.silu(gc.astype(jnp.float32)), od * jax.nn.silu(gd.astype(jnp.float32))], axis=-1)
    return x + jnp.einsum('btc,cd->btd', y.astype(x.dtype), w_out).astype(x.dtype)


def trunk(x, norm_g, final_norm_g, even_w_in, even_w_out, sink_logit, odd_w_in, odd_w_out,
          diff_lambda, diff_subln_g, gla_w_gate2, gla_b_gate, gla_norm_g):
    for i in range(DEPTH):
        j = i // 2
        if i % 2 == 0:
            x = even_layer(x, norm_g[i], even_w_in[j], even_w_out[j], sink_logit[j])
        else:
            x = odd_layer(x, norm_g[i], odd_w_in[j], odd_w_out[j], diff_lambda[j], diff_subln_g[j],
                          gla_w_gate2[j], gla_b_gate[j], gla_norm_g[j], i)
    return rmsnorm(x, final_norm_g)


def setup_inputs(seed: int = 0) -> dict:
    key = jax.random.key(seed)
    ks = jax.random.split(key, 14)
    nrm = lambda k, s: jax.random.normal(k, s, jnp.float32)
    return {
        "x_prompt": nrm(ks[0], (BATCH, SEQ, D_MODEL)),
        "x_sample": nrm(ks[1], (DEC_BATCH, DEC_SEQ, D_MODEL)),
        "norm_g": 1.0 + 0.01 * nrm(ks[2], (DEPTH, D_MODEL)),
        "final_norm_g": 1.0 + 0.01 * nrm(ks[3], (D_MODEL,)),
        "even_w_in": nrm(ks[4], (N_EVEN, D_MODEL, EVEN_COLS)) * D_MODEL ** -0.5,
        "even_w_out": nrm(ks[5], (N_EVEN, EVEN_WIDTH, D_MODEL)) * EVEN_WIDTH ** -0.5,
        "sink_logit": nrm(ks[6], (N_EVEN, A_HEADS)),
        "odd_w_in": nrm(ks[7], (N_ODD, D_MODEL, ODD_COLS)) * D_MODEL ** -0.5,
        "odd_w_out": nrm(ks[8], (N_ODD, ODD_WIDTH, D_MODEL)) * ODD_WIDTH ** -0.5,
        "diff_lambda": 0.1 * nrm(ks[9], (N_ODD, 4, C_QK_DIM)),
        "diff_subln_g": 1.0 + 0.01 * nrm(ks[10], (N_ODD, C_V_DIM)),
        "gla_w_gate2": nrm(ks[11], (N_ODD, 2, D_GATE_RANK, D_KEY_WIDTH)) * D_GATE_RANK ** -0.5,
        "gla_b_gate": 0.1 * nrm(ks[12], (N_ODD, 2, D_KEY_WIDTH)),
        "gla_norm_g": 1.0 + 0.01 * nrm(ks[13], (N_ODD, D_V_DIM)),
    }


def reference(x_prompt, x_sample, norm_g, final_norm_g, even_w_in, even_w_out, sink_logit,
              odd_w_in, odd_w_out, diff_lambda, diff_subln_g, gla_w_gate2, gla_b_gate, gla_norm_g):
    y_prompt = trunk(x_prompt, norm_g, final_norm_g, even_w_in, even_w_out, sink_logit, odd_w_in, odd_w_out,
                     diff_lambda, diff_subln_g, gla_w_gate2, gla_b_gate, gla_norm_g)
    y_sample = trunk(x_sample, norm_g, final_norm_g, even_w_in, even_w_out, sink_logit, odd_w_in, odd_w_out,
                     diff_lambda, diff_subln_g, gla_w_gate2, gla_b_gate, gla_norm_g)
    return (y_prompt, y_sample)
```

```python
import functools
import math

import jax
import jax.numpy as jnp
from jax import lax
from jax.experimental import pallas as pl
from jax.experimental.pallas import tpu as pltpu

F32 = jnp.float32
BF16 = jnp.bfloat16

D_MODEL = 1024
DEPTH = 4
HEAD_DIM = 64
EPS = 1e-6
LANES = 128

A_HEADS = 16
A_KV_HEADS = 4
A_RADIUS = 128
B_PAIRS = ((128, 1), (512, 4), (2048, 16))
B_HEADS = 8
C_HEADS = 8
C_QK_DIM = 64
C_V_DIM = 128
D_HEADS = 4
D_K_DIM = 128
D_V_DIM = 128
D_GATE_RANK = 16
D_GATE_TAU = 16.0
D_CHUNK = 64

A_WIDTH = A_HEADS * HEAD_DIM
A_KV_WIDTH = A_KV_HEADS * HEAD_DIM
B_WIDTH = B_HEADS * HEAD_DIM
B_QKV_WIDTH = len(B_PAIRS) * B_WIDTH
EVEN_COLS = 2 * A_WIDTH + 2 * A_KV_WIDTH + 3 * B_QKV_WIDTH + B_WIDTH
C_QK_WIDTH = C_HEADS * 2 * C_QK_DIM
C_WIDTH = C_HEADS * C_V_DIM
D_KEY_WIDTH = D_HEADS * D_K_DIM
D_WIDTH = D_HEADS * D_V_DIM
ODD_COLS = 2 * C_QK_WIDTH + 2 * C_WIDTH + 2 * D_KEY_WIDTH + 2 * D_WIDTH + 2 * D_GATE_RANK
ODD_COLS_PAD = -(-ODD_COLS // LANES) * LANES

E_QA, E_KA, E_VA, E_GA = 0, A_WIDTH, A_WIDTH + A_KV_WIDTH, A_WIDTH + 2 * A_KV_WIDTH
E_QB = E_GA + A_WIDTH
E_KB = E_QB + B_QKV_WIDTH
E_VB = E_KB + B_QKV_WIDTH
E_GB = E_VB + B_QKV_WIDTH
O_QC, O_KC, O_VC, O_GC = 0, C_QK_WIDTH, 2 * C_QK_WIDTH, 2 * C_QK_WIDTH + C_WIDTH
O_QD = O_GC + C_WIDTH
O_KD = O_QD + D_KEY_WIDTH
O_VD = O_KD + D_KEY_WIDTH
O_GD = O_VD + D_WIDTH
O_AD = O_GD + D_WIDTH

VMEM_LIMIT = 56 * 1024 * 1024


def _params(*sem):
    return pltpu.CompilerParams(dimension_semantics=sem, vmem_limit_bytes=VMEM_LIMIT)


def _silu(g):
    return g / (1.0 + jnp.exp(-g))


def _alibi_slopes(n):
    return jnp.asarray([2.0 ** (-8.0 * (i + 1) / n) for i in range(n)], F32)


def _lane_lo(shape):
    return lax.broadcasted_iota(jnp.int32, shape, len(shape) - 1) < HEAD_DIM


def _proj_in_kernel(x_ref, g_ref, w_ref, z_ref, h_ref):
    @pl.when(pl.program_id(1) == 0)
    def _():
        x = x_ref[...]
        ms = jnp.mean(x * x, axis=-1, keepdims=True)
        h_ref[...] = (x * lax.rsqrt(ms + EPS) * g_ref[...]).astype(BF16)

    z_ref[...] = jnp.dot(h_ref[...], w_ref[...], preferred_element_type=F32).astype(z_ref.dtype)


def _proj_in(x, g, w, tn):
    n, d = x.shape
    c = w.shape[1]
    tm = 1024
    return pl.pallas_call(
        _proj_in_kernel,
        grid=(n // tm, c // tn),
        in_specs=[
            pl.BlockSpec((tm, d), lambda i, j: (i, 0)),
            pl.BlockSpec((1, d), lambda i, j: (0, 0)),
            pl.BlockSpec((d, tn), lambda i, j: (0, j)),
        ],
        out_specs=pl.BlockSpec((tm, tn), lambda i, j: (i, j)),
        out_shape=jax.ShapeDtypeStruct((n, c), BF16),
        scratch_shapes=[pltpu.VMEM((tm, d), BF16)],
        compiler_params=_params("parallel", "arbitrary"),
        name="proj_in",
    )(x, g.reshape(1, d), w)


def _proj_out_kernel(x_ref, y1_ref, y2_ref, w1_ref, w2_ref, *rest, final):
    if final:
        g_ref, o_ref = rest
    else:
        (o_ref,) = rest
    acc = jnp.dot(y1_ref[...], w1_ref[...], preferred_element_type=F32)
    acc = acc + jnp.dot(y2_ref[...], w2_ref[...], preferred_element_type=F32)
    x = x_ref[...] + acc
    if final:
        ms = jnp.mean(x * x, axis=-1, keepdims=True)
        x = x * lax.rsqrt(ms + EPS) * g_ref[...]
    o_ref[...] = x


def _proj_out(x, y1, y2, w1, w2, final_g=None):
    n, d = x.shape
    c1, c2 = y1.shape[1], y2.shape[1]
    tm = 512
    final = final_g is not None
    in_specs = [
        pl.BlockSpec((tm, d), lambda i: (i, 0)),
        pl.BlockSpec((tm, c1), lambda i: (i, 0)),
        pl.BlockSpec((tm, c2), lambda i: (i, 0)),
        pl.BlockSpec((c1, d), lambda i: (0, 0)),
        pl.BlockSpec((c2, d), lambda i: (0, 0)),
    ]
    args = [x, y1, y2, w1, w2]
    if final:
        in_specs.append(pl.BlockSpec((1, d), lambda i: (0, 0)))
        args.append(final_g.reshape(1, d))
    return pl.pallas_call(
        functools.partial(_proj_out_kernel, final=final),
        grid=(n // tm,),
        in_specs=in_specs,
        out_specs=pl.BlockSpec((tm, d), lambda i: (i, 0)),
        out_shape=jax.ShapeDtypeStruct((n, d), F32),
        compiler_params=_params("parallel"),
        name="proj_out",
    )(*args)


def _attn_win_kernel(sink_ref, slope_ref, q_ref, k_ref, v_ref, g_ref, o_ref, *, seq):
    blk = A_RADIUS
    p = pl.program_id(1)
    i = pl.program_id(2)
    ks = pl.multiple_of(jnp.clip((i - 1) * blk, 0, seq - 3 * blk), blk)
    k = k_ref[pl.ds(ks, 3 * blk), :]
    v = v_ref[pl.ds(ks, 3 * blk), :]

    rel = (lax.broadcasted_iota(jnp.int32, (blk, 3 * blk), 1) + ks) - (
        lax.broadcasted_iota(jnp.int32, (blk, 3 * blk), 0) + i * blk)
    dist = jnp.abs(rel)
    pen = jnp.where(dist <= A_RADIUS, dist.astype(F32), 1e30)
    lo = _lane_lo((blk, LANES))

    for j in range(4):
        qf = q_ref[:, j * LANES:(j + 1) * LANES].astype(F32) * (HEAD_DIM ** -0.5)
        qsw = pltpu.roll(qf, HEAD_DIM, axis=1)
        kv_lo = j < 2
        res = []
        for half in range(2):
            head = p * 8 + 2 * j + half
            slope = slope_ref[head]
            q_in_lo = half == 0
            src = qf if q_in_lo == kv_lo else qsw
            qm = jnp.where(lo if kv_lo else ~lo, src, 0.0).astype(BF16)
            s = lax.dot_general(qm, k, (((1,), (1,)), ((), ())), preferred_element_type=F32)
            s = s - slope * pen
            sk = sink_ref[head]
            m = jnp.maximum(jnp.max(s, axis=-1, keepdims=True), sk)
            e = jnp.exp(s - m)
            denom = jnp.sum(e, axis=-1, keepdims=True) + jnp.exp(sk - m)
            pv = jnp.dot(e.astype(BF16), v, preferred_element_type=F32) / denom
            if q_in_lo != kv_lo:
                pv = pltpu.roll(pv, HEAD_DIM, axis=1)
            res.append(pv)
        o = jnp.where(lo, res[0], res[1])
        g = g_ref[:, j * LANES:(j + 1) * LANES].astype(F32)
        o_ref[:, j * LANES:(j + 1) * LANES] = (o * _silu(g)).astype(o_ref.dtype)


def _attn_win(z, sink, batch, seq):
    blk = A_RADIUS
    nt = seq // blk
    w = 4 * LANES
    return pl.pallas_call(
        functools.partial(_attn_win_kernel, seq=seq),
        grid=(batch, 2, nt),
        in_specs=[
            pl.BlockSpec(memory_space=pltpu.SMEM),
            pl.BlockSpec(memory_space=pltpu.SMEM),
            pl.BlockSpec((blk, w), lambda b, p, i: (b * nt + i, E_QA // w + p)),
            pl.BlockSpec((seq, LANES), lambda b, p, i: (b, E_KA // LANES + p)),
            pl.BlockSpec((seq, LANES), lambda b, p, i: (b, E_VA // LANES + p)),
            pl.BlockSpec((blk, w), lambda b, p, i: (b * nt + i, E_GA // w + p)),
        ],
        out_specs=pl.BlockSpec((blk, w), lambda b, p, i: (b * nt + i, p)),
        out_shape=jax.ShapeDtypeStruct((batch * seq, A_WIDTH), BF16),
        compiler_params=_params("parallel", "parallel", "arbitrary"),
        name="attn_win",
    )(sink, _alibi_slopes(A_HEADS), z, z, z, z)


def _attn_dil_kernel(q_ref, k_ref, v_ref, o_ref, lse_ref, *, length, dil, qtile):
    blk = HEAD_DIM
    kw = min(3 * blk, length)
    t = pl.program_id(1)
    lo = _lane_lo((blk, LANES))
    for n in range(qtile // blk):
        q0 = t * qtile + n * blk
        ks = pl.multiple_of(jnp.clip(q0 - blk, 0, length - kw), blk)
        rel = (lax.broadcasted_iota(jnp.int32, (blk, kw), 1) + ks) - (
            lax.broadcasted_iota(jnp.int32, (blk, kw), 0) + q0)
        dist = jnp.abs(rel)
        pen = jnp.where(dist <= blk, dist.astype(F32) * float(dil), 1e30)
        for j in range(B_HEADS // 2):
            cs = slice(j * LANES, (j + 1) * LANES)
            k = k_ref[pl.ds(ks, kw), cs]
            v = v_ref[pl.ds(ks, kw), cs]
            qf = q_ref[n * blk:(n + 1) * blk, cs].astype(F32) * (HEAD_DIM ** -0.5)
            outs, lses = [], []
            for half in range(2):
                head = 2 * j + half
                slope = 2.0 ** (-8.0 * (head + 1) / B_HEADS)
                qm = jnp.where(lo if half == 0 else ~lo, qf, 0.0).astype(BF16)
                s = lax.dot_general(qm, k, (((1,), (1,)), ((), ())), preferred_element_type=F32)
                s = s - slope * pen
                m = jnp.max(s, axis=-1, keepdims=True)
                e = jnp.exp(s - m)
                denom = jnp.sum(e, axis=-1, keepdims=True)
                outs.append(jnp.dot(e.astype(BF16), v, preferred_element_type=F32) / denom)
                lses.append(m + jnp.log(denom))
            o_ref[n * blk:(n + 1) * blk, cs] = jnp.where(lo, outs[0], outs[1])
            lse_ref[n * blk:(n + 1) * blk, cs] = jnp.where(lo, lses[0], lses[1])


def _attn_dil(q, k, v, nseq, length, dil):
    qtile = min(length, 512)
    nt = length // qtile
    w = B_WIDTH
    return pl.pallas_call(
        functools.partial(_attn_dil_kernel, length=length, dil=dil, qtile=qtile),
        grid=(nseq, nt),
        in_specs=[
            pl.BlockSpec((qtile, w), lambda s, t: (s * nt + t, 0)),
            pl.BlockSpec((length, w), lambda s, t: (s, 0)),
            pl.BlockSpec((length, w), lambda s, t: (s, 0)),
        ],
        out_specs=[
            pl.BlockSpec((qtile, w), lambda s, t: (s * nt + t, 0)),
            pl.BlockSpec((qtile, w), lambda s, t: (s * nt + t, 0)),
        ],
        out_shape=[
            jax.ShapeDtypeStruct((nseq * length, w), F32),
            jax.ShapeDtypeStruct((nseq * length, w), F32),
        ],
        compiler_params=_params("parallel", "arbitrary"),
        name="attn_dil",
    )(q, k, v)


def _merge_dil_kernel(o0, o1, o2, l0, l1, l2, g_ref, y_ref):
    a, b, c = l0[...], l1[...], l2[...]
    m = jnp.maximum(jnp.maximum(a, b), c)
    ea, eb, ec = jnp.exp(a - m), jnp.exp(b - m), jnp.exp(c - m)
    tot = ea + eb + ec
    o = (ea * o0[...] + eb * o1[...] + ec * o2[...]) / tot
    y_ref[...] = (o * _silu(g_ref[...].astype(F32))).astype(y_ref.dtype)


def _merge_dil(outs, lses, z):
    n = z.shape[0]
    tm = 512
    w = B_WIDTH
    spec = pl.BlockSpec((tm, w), lambda i: (i, 0))
    return pl.pallas_call(
        _merge_dil_kernel,
        grid=(n // tm,),
        in_specs=[spec] * 6 + [pl.BlockSpec((tm, w), lambda i: (i, E_GB // w))],
        out_specs=spec,
        out_shape=jax.ShapeDtypeStruct((n, w), BF16),
        compiler_params=_params("parallel"),
        name="merge_dil",
    )(*outs, *lses, z)


def _attn_diff_kernel(slope_ref, lam_ref, q_ref, k_ref, v_ref, g_ref, sg_ref, o_ref, *, seq, kchunk, lam_init):
    blk = 128
    h = pl.program_id(1)
    i = pl.program_id(2)
    slope = slope_ref[h]
    lp = lam_ref[...]
    lam = (jnp.exp(jnp.sum(lp[0:1] * lp[1:2], axis=-1, keepdims=True))
           - jnp.exp(jnp.sum(lp[2:3] * lp[3:4], axis=-1, keepdims=True)) + lam_init)
    lo = _lane_lo((blk, LANES))
    qf = q_ref[...].astype(F32) * (C_QK_DIM ** -0.5)
    qs = (jnp.where(lo, qf, 0.0).astype(BF16), jnp.where(lo, 0.0, qf).astype(BF16))

    ms = [jnp.full((blk, 1), -jnp.inf, F32)] * 2
    ls = [jnp.zeros((blk, 1), F32)] * 2
    accs = [jnp.zeros((blk, C_V_DIM), F32)] * 2
    for c in range(seq // kchunk):
        k = k_ref[c * kchunk:(c + 1) * kchunk, :]
        v = v_ref[c * kchunk:(c + 1) * kchunk, :]
        rel = (lax.broadcasted_iota(jnp.int32, (blk, kchunk), 1) + c * kchunk) - (
            lax.broadcasted_iota(jnp.int32, (blk, kchunk), 0) + i * blk)
        pen = slope * jnp.abs(rel).astype(F32)
        for u in range(2):
            s = lax.dot_general(qs[u], k, (((1,), (1,)), ((), ())), preferred_element_type=F32) - pen
            m_new = jnp.maximum(ms[u], jnp.max(s, axis=-1, keepdims=True))
            alpha = jnp.exp(ms[u] - m_new)
            e = jnp.exp(s - m_new)
            ls[u] = alpha * ls[u] + jnp.sum(e, axis=-1, keepdims=True)
            accs[u] = alpha * accs[u] + jnp.dot(e.astype(BF16), v, preferred_element_type=F32)
            ms[u] = m_new
    o = accs[0] / ls[0] - lam * (accs[1] / ls[1])
    var = jnp.mean(o * o, axis=-1, keepdims=True)
    o = o * lax.rsqrt(var + EPS) * sg_ref[...] * (1.0 - lam_init)
    o_ref[...] = (o * _silu(g_ref[...].astype(F32))).astype(o_ref.dtype)


def _attn_diff(z, slopes, lam_p, subln_g, batch, seq, lam_init):
    blk = 128
    nt = seq // blk
    kchunk = min(seq, 2048)
    return pl.pallas_call(
        functools.partial(_attn_diff_kernel, seq=seq, kchunk=kchunk, lam_init=lam_init),
        grid=(batch, C_HEADS, nt),
        in_specs=[
            pl.BlockSpec(memory_space=pltpu.SMEM),
            pl.BlockSpec((4, C_QK_DIM), lambda b, h, i: (0, 0)),
            pl.BlockSpec((blk, LANES), lambda b, h, i: (b * nt + i, O_QC // LANES + h)),
            pl.BlockSpec((seq, LANES), lambda b, h, i: (b, O_KC // LANES + h)),
            pl.BlockSpec((seq, LANES), lambda b, h, i: (b, O_VC // LANES + h)),
            pl.BlockSpec((blk, LANES), lambda b, h, i: (b * nt + i, O_GC // LANES + h)),
            pl.BlockSpec((1, C_V_DIM), lambda b, h, i: (0, 0)),
        ],
        out_specs=pl.BlockSpec((blk, LANES), lambda b, h, i: (b * nt + i, h)),
        out_shape=jax.ShapeDtypeStruct((batch * seq, C_WIDTH), BF16),
        compiler_params=_params("parallel", "parallel", "arbitrary"),
        name="attn_diff",
    )(slopes, lam_p, z, z, z, z, subln_g.reshape(1, C_V_DIM))


def _gla_chunk(q, k, v, ad, wg, bg, st_ref, tri, mask):
    pre = jnp.dot(ad, wg, preferred_element_type=F32) + bg
    log_a = (jnp.minimum(pre, 0.0) - jnp.log(1.0 + jnp.exp(-jnp.abs(pre)))) / D_GATE_TAU
    b = jnp.dot(tri, log_a, preferred_element_type=F32, precision=lax.Precision.HIGHEST)
    b_tot = jnp.sum(log_a, axis=0, keepdims=True)
    qf = q.astype(F32) * (D_K_DIM ** -0.5)
    kf = k.astype(F32)
    q_e = (qf * jnp.exp(b)).astype(BF16)
    k_e = (kf * jnp.exp(-b)).astype(BF16)
    k_s = (kf * jnp.exp(b_tot - b)).astype(BF16)
    a = lax.dot_general(q_e, k_e, (((1,), (1,)), ((), ())), preferred_element_type=F32) * mask
    intra = jnp.dot(a.astype(BF16), v, preferred_element_type=F32)
    st = st_ref[...]
    inter = lax.dot_general(q_e, st.astype(BF16), (((1,), (1,)), ((), ())), preferred_element_type=F32)
    kv_t = lax.dot_general(v, k_s, (((0,), (0,)), ((), ())), preferred_element_type=F32)
    st_ref[...] = jnp.exp(b_tot) * st + kv_t
    return intra + inter


def _gla_kernel(qf_ref, kf_ref, vf_ref, af_ref, qb_ref, kb_ref, vb_ref, ab_ref, wg_ref, bg_ref,
                of_ref, ob_ref, sf_ref, sb_ref, *, tb):
    c = D_CHUNK

    @pl.when(pl.program_id(2) == 0)
    def _():
        sf_ref[...] = jnp.zeros_like(sf_ref)
        sb_ref[...] = jnp.zeros_like(sb_ref)

    row = lax.broadcasted_iota(jnp.int32, (c, c), 0)
    col = lax.broadcasted_iota(jnp.int32, (c, c), 1)
    tri_f = (col <= row).astype(F32)
    tri_b = (col >= row).astype(F32)
    mask_f = tri_f
    mask_b = (col > row).astype(F32)
    wg_f, wg_b = wg_ref[0, 0], wg_ref[1, 0]
    bg_f, bg_b = bg_ref[0, 0], bg_ref[1, 0]
    nch = tb // c
    for n in range(nch):
        r = slice(n * c, (n + 1) * c)
        of_ref[r, :] = _gla_chunk(qf_ref[r, :], kf_ref[r, :], vf_ref[r, :], af_ref[r, :], wg_f, bg_f,
                                  sf_ref, tri_f, mask_f)
    for n in reversed(range(nch)):
        r = slice(n * c, (n + 1) * c)
        ob_ref[r, :] = _gla_chunk(qb_ref[r, :], kb_ref[r, :], vb_ref[r, :], ab_ref[r, :], wg_b, bg_b,
                                  sb_ref, tri_b, mask_b)


def _gla(z, wg, bg, batch, seq):
    tb = 256
    nb = seq // tb
    fwd = lambda off: pl.BlockSpec((tb, LANES), lambda b, h, n: (b * nb + n, off // LANES + h))
    bwd = lambda off: pl.BlockSpec((tb, LANES), lambda b, h, n: (b * nb + nb - 1 - n, off // LANES + h))
    fwd_a = pl.BlockSpec((tb, LANES), lambda b, h, n: (b * nb + n, O_AD // LANES))
    bwd_a = pl.BlockSpec((tb, LANES), lambda b, h, n: (b * nb + nb - 1 - n, O_AD // LANES))
    out_shape = jax.ShapeDtypeStruct((batch * seq, D_WIDTH), F32)
    return pl.pallas_call(
        functools.partial(_gla_kernel, tb=tb),
        grid=(batch, D_HEADS, nb),
        in_specs=[
            fwd(O_QD), fwd(O_KD), fwd(O_VD), fwd_a,
            bwd(O_QD), bwd(O_KD), bwd(O_VD), bwd_a,
            pl.BlockSpec((2, 1, LANES, D_K_DIM), lambda b, h, n: (0, h, 0, 0)),
            pl.BlockSpec((2, 1, 1, D_K_DIM), lambda b, h, n: (0, h, 0, 0)),
        ],
        out_specs=[
            pl.BlockSpec((tb, LANES), lambda b, h, n: (b * nb + n, h)),
            pl.BlockSpec((tb, LANES), lambda b, h, n: (b * nb + nb - 1 - n, h)),
        ],
        out_shape=[out_shape, out_shape],
        scratch_shapes=[pltpu.VMEM((D_V_DIM, D_K_DIM), F32), pltpu.VMEM((D_V_DIM, D_K_DIM), F32)],
        compiler_params=_params("parallel", "parallel", "arbitrary"),
        name="gla",
    )(z, z, z, z, z, z, z, z, wg, bg)


def _merge_gla_kernel(of_ref, ob_ref, g_ref, ng_ref, y_ref):
    for h in range(D_HEADS):
        cs = slice(h * D_V_DIM, (h + 1) * D_V_DIM)
        o = of_ref[:, cs] + ob_ref[:, cs]
        var = jnp.mean(o * o, axis=-1, keepdims=True)
        o = o * lax.rsqrt(var + EPS) * ng_ref[...]
        y_ref[:, cs] = (o * _silu(g_ref[:, cs].astype(F32))).astype(y_ref.dtype)


def _merge_gla(o_f, o_b, z, norm_g):
    n = z.shape[0]
    tm = 512
    w = D_WIDTH
    spec = pl.BlockSpec((tm, w), lambda i: (i, 0))
    return pl.pallas_call(
        _merge_gla_kernel,
        grid=(n // tm,),
        in_specs=[spec, spec, pl.BlockSpec((tm, w), lambda i: (i, O_GD // w)),
                  pl.BlockSpec((1, D_V_DIM), lambda i: (0, 0))],
        out_specs=spec,
        out_shape=jax.ShapeDtypeStruct((n, w), BF16),
        compiler_params=_params("parallel"),
        name="merge_gla",
    )(o_f, o_b, z, norm_g.reshape(1, D_V_DIM))


def _even_layer(x, batch, seq, g, w_in, w_out, sink, final_g=None):
    z = _proj_in(x, g, w_in.astype(BF16), tn=1280)
    ya = _attn_win(z, sink.astype(F32), batch, seq)
    outs, lses = [], []
    for p, (window, dil) in enumerate(B_PAIRS):
        length = seq // dil

        def gather(off):
            a = z[:, off + p * B_WIDTH: off + (p + 1) * B_WIDTH]
            a = a.reshape(batch, length, dil, B_WIDTH).transpose(0, 2, 1, 3)
            return a.reshape(batch * dil * length, B_WIDTH)

        o, lse = _attn_dil(gather(E_QB), gather(E_KB), gather(E_VB), batch * dil, length, dil)

        def scatter(a):
            a = a.reshape(batch, dil, length, B_WIDTH).transpose(0, 2, 1, 3)
            return a.reshape(batch * seq, B_WIDTH)

        outs.append(scatter(o))
        lses.append(scatter(lse))
    yb = _merge_dil(outs, lses, z)
    w_out = w_out.astype(BF16)
    return _proj_out(x, ya, yb, w_out[:A_WIDTH], w_out[A_WIDTH:], final_g)


def _odd_layer(x, batch, seq, g, w_in, w_out, lam_p, subln_g, w_gate2, b_gate, gla_g, layer_idx, final_g=None):
    w_in = jnp.pad(w_in, ((0, 0), (0, ODD_COLS_PAD - ODD_COLS))).astype(BF16)
    z = _proj_in(x, g, w_in, tn=896)
    lam_init = 0.8 - 0.6 * math.exp(-0.3 * layer_idx)
    yc = _attn_diff(z, _alibi_slopes(C_HEADS),lam_p.astype(F32), subln_g.astype(F32), batch, seq, lam_init)
    wg = jnp.zeros((2, LANES, D_KEY_WIDTH), F32)
    for p in range(2):
        wg = wg.at[p, p * D_GATE_RANK:(p + 1) * D_GATE_RANK].set(w_gate2[p].astype(F32))
    wg = wg.reshape(2, LANES, D_HEADS, D_K_DIM).transpose(0, 2, 1, 3).astype(BF16)
    bg = b_gate.astype(F32).reshape(2, D_HEADS, 1, D_K_DIM)
    o_f, o_b = _gla(z, wg, bg, batch, seq)
    yd = _merge_gla(o_f, o_b, z, gla_g.astype(F32))
    w_out = w_out.astype(BF16)
    return _proj_out(x, yc, yd, w_out[:C_WIDTH], w_out[C_WIDTH:], final_g)


def _trunk(x, norm_g, final_norm_g, even_w_in, even_w_out, sink_logit, odd_w_in, odd_w_out,
           diff_lambda, diff_subln_g, gla_w_gate2, gla_b_gate, gla_norm_g):
    batch, seq, d = x.shape
    x = x.reshape(batch * seq, d)
    for i in range(DEPTH):
        j = i // 2
        final_g = final_norm_g if i == DEPTH - 1 else None
        if i % 2 == 0:
            x = _even_layer(x, batch, seq, norm_g[i], even_w_in[j], even_w_out[j], sink_logit[j], final_g)
        else:
            x = _odd_layer(x, batch, seq, norm_g[i], odd_w_in[j], odd_w_out[j], diff_lambda[j], diff_subln_g[j],
                           gla_w_gate2[j], gla_b_gate[j], gla_norm_g[j], i, final_g)
    return x.reshape(batch, seq, d)


def kernel(x_prompt, x_sample, norm_g, final_norm_g, even_w_in, even_w_out, sink_logit, odd_w_in, odd_w_out,
           diff_lambda, diff_subln_g, gla_w_gate2, gla_b_gate, gla_norm_g):
    weights = (norm_g, final_norm_g, even_w_in, even_w_out, sink_logit, odd_w_in, odd_w_out,
               diff_lambda, diff_subln_g, gla_w_gate2, gla_b_gate, gla_norm_g)
    return (_trunk(x_prompt, *weights), _trunk(x_sample, *weights))
```
